```python
import jax, jax.numpy as jnp
from jax import lax
import numpy as np

D_MODEL = 1024
BATCH = 4
SEQ = 4096
DEPTH = 2

PLE_DIM = 256
HEAD_DIM = 64
CONV_WIDTH = 256
CONV_HEADS = CONV_WIDTH // HEAD_DIM
LRU_WIDTH = 512
LRU_HEADS = LRU_WIDTH // HEAD_DIM
SGU_WIDTH = 256
SGU_HEADS = SGU_WIDTH // HEAD_DIM
MIX_WIDTH = CONV_WIDTH + LRU_WIDTH + SGU_WIDTH
IN_WIDTH = 2 * CONV_WIDTH + 2 * LRU_WIDTH + 2 * SGU_WIDTH
CONV_K = 31
LRU_CONV_K = 4
LRU_C = 8.0
CHUNK = 128
D_FF = 2816
FFN_CONV_K = 3
EPS = 1e-6
SPLITS = [CONV_WIDTH, 2 * CONV_WIDTH, 2 * CONV_WIDTH + LRU_WIDTH, 2 * CONV_WIDTH + 2 * LRU_WIDTH,
          2 * CONV_WIDTH + 2 * LRU_WIDTH + SGU_WIDTH]

kernel_name = "hybrid_conv_lru_sgu_encoder"


def rmsnorm(x, g):
    xf = x.astype(jnp.float32)
    y = xf * lax.rsqrt(jnp.mean(xf * xf, axis=-1, keepdims=True) + EPS)
    return (y * g.astype(jnp.float32)).astype(x.dtype)


def group_layernorm(x, g, b, n_groups):
    shp = x.shape
    xf = x.astype(jnp.float32).reshape(shp[:-1] + (n_groups, shp[-1] // n_groups))
    mu = jnp.mean(xf, axis=-1, keepdims=True)
    var = jnp.mean(jnp.square(xf - mu), axis=-1, keepdims=True)
    y = ((xf - mu) * lax.rsqrt(var + EPS)).reshape(shp)
    return (y * g.astype(jnp.float32) + b.astype(jnp.float32)).astype(x.dtype)


def dwconv(x, w, b, pad_l, pad_r):
    c = x.shape[-1]
    y = lax.conv_general_dilated(
        x, w[:, None, :].astype(x.dtype), window_strides=(1,), padding=[(pad_l, pad_r)],
        dimension_numbers=("NWC", "WIO", "NWC"), feature_group_count=c)
    return y + b.astype(x.dtype)


def conformer_conv(val, gate, w_dw, b_dw, gn_g, gn_b):
    h = val * jax.nn.sigmoid(gate)
    h = dwconv(h, w_dw, b_dw, CONV_K // 2, CONV_K // 2)
    h = group_layernorm(h, gn_g, gn_b, CONV_HEADS)
    return jax.nn.silu(h)


def _lin_combine(left, right):
    a1, b1 = left
    a2, b2 = right
    return a1 * a2, a2 * b1 + b2


def rglru_direction(x, w_conv, b_conv, w_a, b_a, w_x, b_x, lam):
    bsz, s, _ = x.shape
    xc = dwconv(x, w_conv, b_conv, LRU_CONV_K - 1, 0)
    xh = xc.reshape(bsz, s, LRU_HEADS, HEAD_DIM)
    r = jax.nn.sigmoid(jnp.einsum("bshd,hde->bshe", xh, w_a).reshape(bsz, s, LRU_WIDTH) + b_a)
    i = jax.nn.sigmoid(jnp.einsum("bshd,hde->bshe", xh, w_x).reshape(bsz, s, LRU_WIDTH) + b_x)
    log_a = LRU_C * r.astype(jnp.float32) * jax.nn.log_sigmoid(lam.astype(jnp.float32))
    a = jnp.exp(log_a)
    u = jnp.sqrt(-jnp.expm1(2.0 * log_a)) * (i * xc).astype(jnp.float32)
    _, h = lax.associative_scan(_lin_combine, (a, u), axis=1)
    return h.astype(x.dtype)


def chunked_sgu(u, v, ln_g, ln_b, w_s, b_s):
    bsz, s, _ = v.shape
    u = jax.nn.gelu(u)
    v = group_layernorm(jax.nn.gelu(v), ln_g, ln_b, 1)
    vh = v.reshape(bsz, s // CHUNK, CHUNK, SGU_HEADS, HEAD_DIM)
    mixed = jnp.einsum("hpq,bnqhd->bnphd", w_s, vh) + jnp.transpose(b_s)[None, None, :, :, None]
    return u * mixed.reshape(bsz, s, SGU_WIDTH)


def setup_inputs(seed: int = 0) -> dict:
    key = jax.random.key(seed)
    ks = iter(jax.random.split(key, 40))

    def nrm(shape, scale):
        return jax.random.normal(next(ks), shape, jnp.float32) * scale

    def gain(shape):
        return 1.0 + nrm(shape, 0.02)

    a_pow = jax.random.uniform(next(ks), (DEPTH, 2, LRU_WIDTH), jnp.float32, 0.9, 0.999)
    a_base = a_pow ** (1.0 / LRU_C)
    lam = jnp.log(a_base) - jnp.log1p(-a_base)
    return {
        "x": nrm((BATCH, SEQ, D_MODEL), 1.0),
        "p": nrm((DEPTH, BATCH, SEQ, PLE_DIM), 1.0),
        "norm_mix": gain((DEPTH, D_MODEL)),
        "w_in": nrm((DEPTH, D_MODEL, IN_WIDTH), D_MODEL ** -0.5),
        "conv_dw_w": nrm((DEPTH, CONV_K, CONV_WIDTH), CONV_K ** -0.5),
        "conv_dw_b": nrm((DEPTH, CONV_WIDTH), 0.02),
        "conv_gn_g": gain((DEPTH, CONV_WIDTH)),
        "conv_gn_b": nrm((DEPTH, CONV_WIDTH), 0.02),
        "lru_conv_w": nrm((DEPTH, 2, LRU_CONV_K, LRU_WIDTH), LRU_CONV_K ** -0.5),
        "lru_conv_b": nrm((DEPTH, 2, LRU_WIDTH), 0.02),
        "lru_wa": nrm((DEPTH, 2, LRU_HEADS, HEAD_DIM, HEAD_DIM), HEAD_DIM ** -0.5),
        "lru_ba": nrm((DEPTH, 2, LRU_WIDTH), 0.02),
        "lru_wx": nrm((DEPTH, 2, LRU_HEADS, HEAD_DIM, HEAD_DIM), HEAD_DIM ** -0.5),
        "lru_bx": nrm((DEPTH, 2, LRU_WIDTH), 0.02),
        "lru_lambda": lam,
        "sgu_ln_g": gain((DEPTH, SGU_WIDTH)),
        "sgu_ln_b": nrm((DEPTH, SGU_WIDTH), 0.02),
        "sgu_ws": nrm((DEPTH, SGU_HEADS, CHUNK, CHUNK), CHUNK ** -0.5),
        "sgu_bs": 1.0 + nrm((DEPTH, SGU_HEADS, CHUNK), 0.02),
        "out_norm": gain((DEPTH, MIX_WIDTH)),
        "w_out": nrm((DEPTH, MIX_WIDTH, D_MODEL), MIX_WIDTH ** -0.5),
        "norm_ffn": gain((DEPTH, D_MODEL)),
        "w_up": nrm((DEPTH, D_MODEL, 2 * D_FF), D_MODEL ** -0.5),
        "ffn_conv_w": nrm((DEPTH, FFN_CONV_K, 2 * D_FF), FFN_CONV_K ** -0.5),
        "ffn_conv_b": nrm((DEPTH, 2 * D_FF), 0.02),
        "w_down": nrm((DEPTH, D_FF, D_MODEL), D_FF ** -0.5),
        "norm_ple": gain((DEPTH, D_MODEL)),
        "w_ple_gate": nrm((DEPTH, D_MODEL, D_MODEL), D_MODEL ** -0.5),
        "b_ple_gate": nrm((DEPTH, D_MODEL), 0.02),
        "w_ple": nrm((DEPTH, PLE_DIM, D_MODEL), PLE_DIM ** -0.5),
        "ple_post_norm": gain((DEPTH, D_MODEL)),
        "final_norm": gain((D_MODEL,)),
    }


def reference(x, p, norm_mix, w_in, conv_dw_w, conv_dw_b, conv_gn_g, conv_gn_b,
              lru_conv_w, lru_conv_b, lru_wa, lru_ba, lru_wx, lru_bx, lru_lambda,
              sgu_ln_g, sgu_ln_b, sgu_ws, sgu_bs, out_norm, w_out, norm_ffn, w_up,
              ffn_conv_w, ffn_conv_b, w_down, norm_ple, w_ple_gate, b_ple_gate, w_ple,
              ple_post_norm, final_norm):
    for l in range(DEPTH):
        h = rmsnorm(x, norm_mix[l])
        z = h @ w_in[l]
        cv, cg, lx, lg, su, sv = jnp.split(z, SPLITS, axis=-1)

        ya = conformer_conv(cv, cg, conv_dw_w[l], conv_dw_b[l], conv_gn_g[l], conv_gn_b[l])

        h_fwd = rglru_direction(lx, lru_conv_w[l, 0], lru_conv_b[l, 0], lru_wa[l, 0], lru_ba[l, 0],
                                lru_wx[l, 0], lru_bx[l, 0], lru_lambda[l, 0])
        h_bwd = jnp.flip(rglru_direction(jnp.flip(lx, axis=1), lru_conv_w[l, 1], lru_conv_b[l, 1],
                                         lru_wa[l, 1], lru_ba[l, 1], lru_wx[l, 1], lru_bx[l, 1],
                                         lru_lambda[l, 1]), axis=1)
        yb = (h_fwd + h_bwd) * jax.nn.gelu(lg)

        yc = chunked_sgu(su, sv, sgu_ln_g[l], sgu_ln_b[l], sgu_ws[l], sgu_bs[l])

        g_out = out_norm[l]
        y = jnp.concatenate([
            rmsnorm(ya, g_out[:CONV_WIDTH]),
            rmsnorm(yb, g_out[CONV_WIDTH:CONV_WIDTH + LRU_WIDTH]),
            rmsnorm(yc, g_out[CONV_WIDTH + LRU_WIDTH:]),
        ], axis=-1)
        x = x + y @ w_out[l]

        h = rmsnorm(x, norm_ffn[l])
        f = dwconv(h @ w_up[l], ffn_conv_w[l], ffn_conv_b[l], FFN_CONV_K // 2, FFN_CONV_K // 2)
        fg, fu = jnp.split(f, 2, axis=-1)
        x = x + (jax.nn.gelu(fg) * fu) @ w_down[l]

        gate = jax.nn.sigmoid(rmsnorm(x, norm_ple[l]) @ w_ple_gate[l] + b_ple_gate[l])
        e = rmsnorm(p[l] @ w_ple[l], ple_post_norm[l])
        x = x + gate * e
    return rmsnorm(x, final_norm)
```

```python
import functools
import math

import jax
import jax.numpy as jnp
from jax import lax
from jax.experimental import pallas as pl
from jax.experimental.pallas import tpu as pltpu

D_MODEL = 1024
PLE_DIM = 256
HEAD_DIM = 64
CONV_WIDTH = 256
LRU_WIDTH = 512
LRU_HEADS = LRU_WIDTH // HEAD_DIM
SGU_WIDTH = 256
SGU_HEADS = SGU_WIDTH // HEAD_DIM
MIX_WIDTH = CONV_WIDTH + LRU_WIDTH + SGU_WIDTH
IN_WIDTH = 2 * MIX_WIDTH
CONV_K = 31
LRU_CONV_K = 4
LRU_C = 8.0
CHUNK = 128
D_FF = 2816
FFN_CONV_K = 3
EPS = 1e-6

SUBLANES = 8
SEQ_TILE = 512
CONV_HALO = 16
FF_CHUNK = 256
VMEM_LIMIT = 56 * 1024 * 1024

F32 = jnp.float32
BF16 = jnp.bfloat16


def _sigmoid(x):
    return 0.5 * jnp.tanh(0.5 * x) + 0.5


def _gelu(x):
    return 0.5 * x * (1.0 + jnp.tanh(0.7978845608028654 * (x + 0.044715 * (x * x * x))))


def _rms(x, g):
    return x * lax.rsqrt(jnp.mean(x * x, axis=-1, keepdims=True) + EPS) * g


def _dot(a, b):
    return jnp.dot(a, b, preferred_element_type=F32)


def _split_dot(x, m):
    hi = x.astype(BF16)
    lo = (x - hi.astype(F32)).astype(BF16)
    return _dot(hi, m) + _dot(lo, m)


def _log_sigmoid(x):
    y = jnp.exp(-jnp.abs(x))
    log1p_y = jnp.where(y < 1e-4, y * (1.0 - 0.5 * y), jnp.log(1.0 + y))
    return jnp.minimum(x, 0.0) - log1p_y


def _one_minus_exp(t, exp_t):
    series = -t * (1.0 + 0.5 * t * (1.0 + (1.0 / 3.0) * t * (1.0 + 0.25 * t)))
    return jnp.where(t > -0.03125, series, 1.0 - exp_t)


def _lru_gates(xc, wg_ref, ba, bx, lam):
    xb = xc.astype(BF16)
    half = LRU_WIDTH // 2
    x0, x1 = xb[:, :half], xb[:, half:]
    r_pre = jnp.concatenate([_dot(x0, wg_ref[0]), _dot(x1, wg_ref[1])], axis=1) + ba
    i_pre = jnp.concatenate([_dot(x0, wg_ref[2]), _dot(x1, wg_ref[3])], axis=1) + bx
    r = _sigmoid(r_pre)
    i = _sigmoid(i_pre)
    log_a = (LRU_C * _log_sigmoid(lam)) * r
    a = jnp.exp(log_a)
    u = jnp.sqrt(_one_minus_exp(2.0 * log_a, a * a)) * (i * xc)
    return a, u


def _scan_tile(a_ref, u_ref, carry_ref, n_rows, reverse):
    width = u_ref.shape[1]
    rows = lax.broadcasted_iota(jnp.int32, (SUBLANES, width), 0)
    n_blocks = n_rows // SUBLANES

    def body(j, carry):
        blk = (n_blocks - 1 - j) if reverse else j
        r0 = pl.multiple_of(blk * SUBLANES, SUBLANES)
        a = a_ref[pl.ds(r0, SUBLANES), :]
        u = u_ref[pl.ds(r0, SUBLANES), :]
        for s in (1, 2, 4):
            shift = (SUBLANES - s) if reverse else s
            a_sh = pltpu.roll(a, shift, 0)
            u_sh = pltpu.roll(u, shift, 0)
            m = (rows < SUBLANES - s) if reverse else (rows >= s)
            u = jnp.where(m, a * u_sh + u, u)
            a = jnp.where(m, a * a_sh, a)
        h = a * carry + u
        u_ref[pl.ds(r0, SUBLANES), :] = h
        return h[0:1, :] if reverse else h[SUBLANES - 1:SUBLANES, :]

    carry_ref[...] = lax.fori_loop(0, n_blocks, body, carry_ref[...])


def _lru_bwd_kernel(x_ref, nm_ref, w_ref, cw_ref, cb_ref, wg_ref, ba_ref, bx_ref, lam_ref,
                    out_ref, lx_buf, a_buf, carry):
    tile = x_ref.shape[0]
    first = pl.program_id(1) == 0

    @pl.when(first)
    def _():
        lx_buf[pl.ds(tile, SUBLANES), :] = jnp.zeros((SUBLANES, LRU_WIDTH), F32)
        carry[...] = jnp.zeros_like(carry)

    h = _rms(x_ref[...], nm_ref[...]).astype(BF16)
    lx_buf[pl.ds(0, tile), :] = _dot(h, w_ref[...])
    xc = cb_ref[...] + cw_ref[LRU_CONV_K - 1:LRU_CONV_K, :] * lx_buf[pl.ds(0, tile), :]
    for k in range(LRU_CONV_K - 1):
        xc = xc + cw_ref[k:k + 1, :] * lx_buf[pl.ds(LRU_CONV_K - 1 - k, tile), :]
    lx_buf[pl.ds(tile, SUBLANES), :] = lx_buf[pl.ds(0, SUBLANES), :]
    a, u = _lru_gates(xc, wg_ref, ba_ref[...], bx_ref[...], lam_ref[...])
    a_buf[...] = a
    out_ref[...] = u
    _scan_tile(a_buf, out_ref, carry, tile, reverse=True)


def _mixer_kernel(x_ref, xn_ref, hb_ref, nm_ref, w_in_ref,
                  cw_ref, cb_ref, gng_ref, gnb_ref, gm_ref,
                  lcw_ref, lcb_ref, wg_ref, ba_ref, bx_ref, lam_ref,
                  sg_ref, sb_ref, wcat_ref, bs_ref, on_ref, w_out_ref,
                  out_ref, g_buf, lx_buf, a_buf, h_buf, carry):
    tile = x_ref.shape[0]
    step = pl.program_id(1)
    first = step == 0
    last = step == pl.num_programs(1) - 1

    @pl.when(first)
    def _():
        g_buf[pl.ds(0, CONV_HALO), :] = jnp.zeros((CONV_HALO, CONV_WIDTH), F32)
        lx_buf[pl.ds(0, SUBLANES), :] = jnp.zeros((SUBLANES, LRU_WIDTH), F32)
        carry[...] = jnp.zeros_like(carry)

    x = x_ref[...]
    nm = nm_ref[...]
    h = _rms(x, nm).astype(BF16)
    z = _dot(h, w_in_ref[...])
    cv = z[:, 0:CONV_WIDTH]
    cg = z[:, CONV_WIDTH:2 * CONV_WIDTH]
    o = 2 * CONV_WIDTH
    lx = z[:, o:o + LRU_WIDTH]
    lg = z[:, o + LRU_WIDTH:o + 2 * LRU_WIDTH]
    o = o + 2 * LRU_WIDTH
    su = z[:, o:o + SGU_WIDTH]
    sv = z[:, o + SGU_WIDTH:o + 2 * SGU_WIDTH]

    hn = _rms(xn_ref[...], nm).astype(BF16)
    zn = _dot(hn, w_in_ref[:, 0:2 * CONV_WIDTH])
    glu_next = zn[:, 0:CONV_WIDTH] * _sigmoid(zn[:, CONV_WIDTH:])
    g_buf[pl.ds(CONV_HALO, tile), :] = cv * _sigmoid(cg)
    g_buf[pl.ds(CONV_HALO + tile, CONV_HALO), :] = jnp.where(last, 0.0, glu_next)
    base = CONV_HALO - CONV_K // 2
    acc = cb_ref[...] + cw_ref[0:1, :] * g_buf[pl.ds(base, tile), :]
    for d in range(1, CONV_K):
        acc = acc + cw_ref[d:d + 1, :] * g_buf[pl.ds(base + d, tile), :]
    g_buf[pl.ds(0, CONV_HALO), :] = g_buf[pl.ds(tile, CONV_HALO), :]
    gm = gm_ref[...]
    dev = acc - _split_dot(acc, gm)
    var = _split_dot(dev * dev, gm)
    ya = dev * lax.rsqrt(var + EPS) * gng_ref[...] + gnb_ref[...]
    ya = ya * _sigmoid(ya)

    lx_buf[pl.ds(SUBLANES, tile), :] = lx
    xc = lcb_ref[...] + lcw_ref[LRU_CONV_K - 1:LRU_CONV_K, :] * lx
    for k in range(LRU_CONV_K - 1):
        off = SUBLANES - (LRU_CONV_K - 1 - k)
        xc = xc + lcw_ref[k:k + 1, :] * lx_buf[pl.ds(off, tile), :]
    lx_buf[pl.ds(0, SUBLANES), :] = lx_buf[pl.ds(tile, SUBLANES), :]
    a, u = _lru_gates(xc, wg_ref, ba_ref[...], bx_ref[...], lam_ref[...])
    a_buf[...] = a
    h_buf[...] = u
    _scan_tile(a_buf, h_buf, carry, tile, reverse=False)
    yb = (h_buf[...] + hb_ref[...]) * _gelu(lg)

    gu = _gelu(su)
    gv = _gelu(sv)
    mu = jnp.mean(gv, axis=-1, keepdims=True)
    dv = gv - mu
    v = dv * lax.rsqrt(jnp.mean(dv * dv, axis=-1, keepdims=True) + EPS) * sg_ref[...] + sb_ref[...]
    head = lax.broadcasted_iota(jnp.int32, (CHUNK, SGU_WIDTH), 1) // HEAD_DIM
    wcat = wcat_ref[...]
    bs = bs_ref[...]
    yc_parts = []
    for c in range(tile // CHUNK):
        vc = v[c * CHUNK:(c + 1) * CHUNK, :]
        stack = jnp.concatenate([jnp.where(head == hd, vc, 0.0) for hd in range(SGU_HEADS)], axis=0)
        mixed = _dot(wcat, stack.astype(BF16)) + bs
        yc_parts.append(gu[c * CHUNK:(c + 1) * CHUNK, :] * mixed)
    yc = jnp.concatenate(yc_parts, axis=0)

    on = on_ref[...]
    y = jnp.concatenate([
        _rms(ya, on[:, 0:CONV_WIDTH]),
        _rms(yb, on[:, CONV_WIDTH:CONV_WIDTH + LRU_WIDTH]),
        _rms(yc, on[:, CONV_WIDTH + LRU_WIDTH:]),
    ], axis=1)
    out_ref[...] = x + _dot(y.astype(BF16), w_out_ref[...])


def _ffn_ple_kernel(x_ref, xp_ref, xn_ref, p_ref, nf_ref, w_up_ref, fw_ref, fb_ref, w_dn_ref,
                    np_ref, wpg_ref, bpg_ref, wp_ref, pn_ref, fn_ref,
                    out_ref, g_buf, u_buf, *, final):
    tile = x_ref.shape[0]
    step = pl.program_id(1)
    first = step == 0
    last = step == pl.num_programs(1) - 1

    x = x_ref[...]
    nf = nf_ref[...]
    hb = jnp.concatenate([
        jnp.where(first, 0.0, _rms(xp_ref[...], nf)),
        _rms(x, nf),
        jnp.where(last, 0.0, _rms(xn_ref[...], nf)),
    ], axis=0).astype(BF16)

    acc = x
    for j in range(D_FF // FF_CHUNK):
        cg = slice(j * FF_CHUNK, (j + 1) * FF_CHUNK)
        cu = slice(D_FF + j * FF_CHUNK, D_FF + (j + 1) * FF_CHUNK)
        g_buf[...] = _dot(hb, w_up_ref[:, cg])
        u_buf[...] = _dot(hb, w_up_ref[:, cu])
        fg = fb_ref[:, cg]
        fu = fb_ref[:, cu]
        for k in range(FFN_CONV_K):
            off = SUBLANES - FFN_CONV_K // 2 + k
            fg = fg + fw_ref[k:k + 1, cg] * g_buf[pl.ds(off, tile), :]
            fu = fu + fw_ref[k:k + 1, cu] * u_buf[pl.ds(off, tile), :]
        act = (_gelu(fg) * fu).astype(BF16)
        acc = acc + _dot(act, w_dn_ref[cg, :])

    gate = _sigmoid(_dot(_rms(acc, np_ref[...]).astype(BF16), wpg_ref[...]) + bpg_ref[...])
    e = _rms(_dot(p_ref[...].astype(BF16), wp_ref[...]), pn_ref[...])
    y = acc + gate * e
    if final:
        y = _rms(y, fn_ref[...])
    out_ref[...] = y


def _const_spec(shape, layer_index):
    nd = len(shape)
    idx = tuple(layer_index) + (0,) * (nd - len(layer_index))
    block = (None,) * len(layer_index) + tuple(shape[len(layer_index):])
    return pl.BlockSpec(block, lambda b, i: idx, pipeline_mode=pl.Buffered(1))


def _block_diag(w):
    n, d, _ = w.shape
    eye = jnp.eye(n, dtype=w.dtype)
    return (eye[:, None, :, None] * w[:, :, None, :]).reshape(n * d, n * d)


def _gate_weights(wa, wx):
    hh = LRU_HEADS // 2
    return jnp.stack([_block_diag(wa[:hh]), _block_diag(wa[hh:]),
                      _block_diag(wx[:hh]), _block_diag(wx[hh:])]).astype(BF16)


def kernel(x, p, norm_mix, w_in, conv_dw_w, conv_dw_b, conv_gn_g, conv_gn_b, lru_conv_w, lru_conv_b, lru_wa, lru_ba, lru_wx, lru_bx, lru_lambda, sgu_ln_g, sgu_ln_b, sgu_ws, sgu_bs, out_norm, w_out, norm_ffn, w_up, ffn_conv_w, ffn_conv_b, w_down, norm_ple, w_ple_gate, b_ple_gate, w_ple, ple_post_norm, final_norm):
    bsz, seq, _ = x.shape
    depth = w_in.shape[0]
    tile = SEQ_TILE
    nt = seq // tile
    assert seq % tile == 0 and tile % CHUNK == 0

    row = lambda a: a[..., None, :]
    w_in_b = w_in.astype(BF16)
    w_out_b = w_out.astype(BF16)
    w_up_b = w_up.astype(BF16)
    w_dn_b = w_down.astype(BF16)
    wpg_b = w_ple_gate.astype(BF16)
    wp_b = w_ple.astype(BF16)
    wg = jnp.stack([jnp.stack([_gate_weights(lru_wa[l, d], lru_wx[l, d]) for d in range(2)])
                    for l in range(depth)])
    wcat = jnp.transpose(sgu_ws, (0, 2, 1, 3)).reshape(depth, CHUNK, SGU_HEADS * CHUNK).astype(BF16)
    bs_full = jnp.repeat(jnp.transpose(sgu_bs, (0, 2, 1)), HEAD_DIM, axis=2)
    grp = jnp.arange(CONV_WIDTH) // HEAD_DIM
    gmean = ((grp[:, None] == grp[None, :]).astype(F32) / HEAD_DIM).astype(BF16)
    norm_mix_r, out_norm_r, norm_ffn_r, norm_ple_r = row(norm_mix), row(out_norm), row(norm_ffn), row(norm_ple)
    conv_b_r, gn_g_r, gn_b_r = row(conv_dw_b), row(conv_gn_g), row(conv_gn_b)
    lcb_r, ba_r, bx_r, lam_r = row(lru_conv_b), row(lru_ba), row(lru_bx), row(lru_lambda)
    sg_r, sb_r = row(sgu_ln_g), row(sgu_ln_b)
    fb_r, bpg_r, pn_r = row(ffn_conv_b), row(b_ple_gate), row(ple_post_norm)
    fn_r = final_norm[None, :]

    params = pltpu.CompilerParams(dimension_semantics=("arbitrary", "arbitrary"),
                                  vmem_limit_bytes=VMEM_LIMIT)
    x_tile = lambda width, imap: pl.BlockSpec((None, tile, width), imap)
    fwd = lambda b, i: (b, i, 0)
    rev = lambda b, i: (b, nt - 1 - i, 0)

    for l in range(depth):
        cs = lambda a, *extra: _const_spec(a.shape, (l,) + extra)

        h_bwd = pl.pallas_call(
            _lru_bwd_kernel,
            grid=(bsz, nt),
            in_specs=[
                x_tile(D_MODEL, rev),
                cs(norm_mix_r),
                pl.BlockSpec((None, D_MODEL, LRU_WIDTH), lambda b, i: (l, 0, 1),
                             pipeline_mode=pl.Buffered(1)),
                cs(lru_conv_w, 1), cs(lcb_r, 1), cs(wg, 1), cs(ba_r, 1), cs(bx_r, 1), cs(lam_r, 1),
            ],
            out_specs=x_tile(LRU_WIDTH, rev),
            out_shape=jax.ShapeDtypeStruct((bsz, seq, LRU_WIDTH), F32),
            scratch_shapes=[
                pltpu.VMEM((tile + SUBLANES, LRU_WIDTH), F32),
                pltpu.VMEM((tile, LRU_WIDTH), F32),
                pltpu.VMEM((1, LRU_WIDTH), F32),
            ],
            compiler_params=params,
            name="lru_bwd",
        )(x, norm_mix_r, w_in_b, lru_conv_w, lcb_r, wg, ba_r, bx_r, lam_r)

        halo_blocks = tile // CONV_HALO
        n_halo = seq // CONV_HALO
        x = pl.pallas_call(
            _mixer_kernel,
            grid=(bsz, nt),
            in_specs=[
                x_tile(D_MODEL, fwd),
                pl.BlockSpec((None, CONV_HALO, D_MODEL),
                             lambda b, i: (b, jnp.minimum((i + 1) * halo_blocks, n_halo - 1), 0)),
                x_tile(LRU_WIDTH, fwd),
                cs(norm_mix_r), cs(w_in_b),
                cs(conv_dw_w), cs(conv_b_r), cs(gn_g_r), cs(gn_b_r),
                pl.BlockSpec(gmean.shape, lambda b, i: (0, 0), pipeline_mode=pl.Buffered(1)),
                cs(lru_conv_w, 0), cs(lcb_r, 0), cs(wg, 0), cs(ba_r, 0), cs(bx_r, 0), cs(lam_r, 0),
                cs(sg_r), cs(sb_r), cs(wcat), cs(bs_full), cs(out_norm_r), cs(w_out_b),
            ],
            out_specs=x_tile(D_MODEL, fwd),
            out_shape=jax.ShapeDtypeStruct((bsz, seq, D_MODEL), F32),
            scratch_shapes=[
                pltpu.VMEM((tile + 2 * CONV_HALO, CONV_WIDTH), F32),
                pltpu.VMEM((tile + SUBLANES, LRU_WIDTH), F32),
                pltpu.VMEM((tile, LRU_WIDTH), F32),
                pltpu.VMEM((tile, LRU_WIDTH), F32),
                pltpu.VMEM((1, LRU_WIDTH), F32),
            ],
            compiler_params=params,
            name="mixer",
        )(x, x, h_bwd, norm_mix_r, w_in_b, conv_dw_w, conv_b_r, gn_g_r, gn_b_r, gmean,
          lru_conv_w, lcb_r, wg, ba_r, bx_r, lam_r, sg_r, sb_r, wcat, bs_full, out_norm_r, w_out_b)

        rows8 = tile // SUBLANES
        n8 = seq // SUBLANES
        x = pl.pallas_call(
            functools.partial(_ffn_ple_kernel, final=(l == depth - 1)),
            grid=(bsz, nt),
            in_specs=[
                x_tile(D_MODEL, fwd),
                pl.BlockSpec((None, SUBLANES, D_MODEL),
                             lambda b, i: (b, jnp.maximum(i * rows8 - 1, 0), 0)),
                pl.BlockSpec((None, SUBLANES, D_MODEL),
                             lambda b, i: (b, jnp.minimum((i + 1) * rows8, n8 - 1), 0)),
                pl.BlockSpec((None, None, tile, PLE_DIM), lambda b, i: (l, b, i, 0)),
                cs(norm_ffn_r), cs(w_up_b), cs(ffn_conv_w), cs(fb_r), cs(w_dn_b),
                cs(norm_ple_r), cs(wpg_b), cs(bpg_r), cs(wp_b), cs(pn_r),
                pl.BlockSpec(fn_r.shape, lambda b, i: (0, 0), pipeline_mode=pl.Buffered(1)),
            ],
            out_specs=x_tile(D_MODEL, fwd),
            out_shape=jax.ShapeDtypeStruct((bsz, seq, D_MODEL), F32),
            scratch_shapes=[
                pltpu.VMEM((tile + 2 * SUBLANES, FF_CHUNK), F32),
                pltpu.VMEM((tile + 2 * SUBLANES, FF_CHUNK), F32),
            ],
            compiler_params=params,
            name="ffn_ple",
        )(x, x, x, p, norm_ffn_r, w_up_b, ffn_conv_w, fb_r, w_dn_b,
          norm_ple_r, wpg_b, bpg_r, wp_b, pn_r, fn_r)
    return x
```

```python
import functools

import jax
import jax.numpy as jnp
from jax import lax
from jax.experimental import pallas as pl
from jax.experimental.pallas import tpu as pltpu

D_MODEL = 1024
PLE_DIM = 256
HEAD_DIM = 64
CONV_WIDTH = 256
LRU_WIDTH = 512
LRU_HEADS = LRU_WIDTH // HEAD_DIM
SGU_WIDTH = 256
SGU_HEADS = SGU_WIDTH // HEAD_DIM
MIX_WIDTH = CONV_WIDTH + LRU_WIDTH + SGU_WIDTH
IN_WIDTH = 2 * MIX_WIDTH
CONV_K = 31
LRU_CONV_K = 4
LRU_C = 8.0
CHUNK = 128
D_FF = 2816
FFN_CONV_K = 3
EPS = 1e-6

LANES = 128
SUBLANES = 8
SEQ_TILE = 512
SEG_LEN = SEQ_TILE // SUBLANES
SEG_PITCH = SEG_LEN + 4
CONV_HALO = 16
CONV_ROWS = 128
FF_CHUNK = 256
VMEM_LIMIT = 56 * 1024 * 1024

F32 = jnp.float32
BF16 = jnp.bfloat16


def _lanes(c):
    return slice(c * LANES, (c + 1) * LANES)


def _sigmoid(x):
    return 0.5 * jnp.tanh(0.5 * x) + 0.5


def _gelu2(x):
    t = jnp.tanh(x * (0.7978845608028654 + (0.7978845608028654 * 0.044715) * (x * x)))
    return x + x * t


def _rms(x, g):
    return x * lax.rsqrt(jnp.mean(x * x, axis=-1, keepdims=True) + EPS) * g


def _dot(a, b):
    return jnp.dot(a, b, preferred_element_type=F32)


def _split_dot(x, m):
    hi = x.astype(BF16)
    lo = (x - hi.astype(F32)).astype(BF16)
    return _dot(hi, m) + _dot(lo, m)


def _log_sigmoid(x):
    y = jnp.exp(-jnp.abs(x))
    log1p_y = jnp.where(y < 1e-4, y * (1.0 - 0.5 * y), jnp.log(1.0 + y))
    return jnp.minimum(x, 0.0) - log1p_y


def _one_minus_exp(t, exp_t):
    series = -t * (1.0 + 0.5 * t * (1.0 + (1.0 / 3.0) * t * (1.0 + 0.25 * t)))
    return jnp.where(t > -0.03125, series, 1.0 - exp_t)


def _sqrt_nonneg(y):
    return y * lax.rsqrt(jnp.maximum(y, 1e-30))


def _lru_gates(xc, wg_ref, ba, bx, lam):
    xb = xc.astype(BF16)
    half = LRU_WIDTH // 2
    x0, x1 = xb[:, :half], xb[:, half:]
    r_pre = jnp.concatenate([_dot(x0, wg_ref[0]), _dot(x1, wg_ref[1])], axis=1) + ba
    i_pre = jnp.concatenate([_dot(x0, wg_ref[2]), _dot(x1, wg_ref[3])], axis=1) + bx
    r = _sigmoid(r_pre)
    i = _sigmoid(i_pre)
    log_a = (LRU_C * _log_sigmoid(lam)) * r
    a = jnp.exp(log_a)
    u = _sqrt_nonneg(_one_minus_exp(2.0 * log_a, a * a)) * (i * xc)
    return a, u


def _store_segments(buf, val):
    for c in range(buf.shape[0]):
        for s in range(SUBLANES):
            buf[c, pl.ds(s * SEG_PITCH, SEG_LEN), :] = val[s * SEG_LEN:(s + 1) * SEG_LEN, _lanes(c)]


def _sublane_scan(a, u, reverse):
    rows = lax.broadcasted_iota(jnp.int32, a.shape, 0)
    for s in (1, 2, 4):
        shift = (SUBLANES - s) if reverse else s
        a_sh = pltpu.roll(a, shift, 0)
        u_sh = pltpu.roll(u, shift, 0)
        m = (rows < SUBLANES - s) if reverse else (rows >= s)
        u = jnp.where(m, a * u_sh + u, u)
        a = jnp.where(m, a * a_sh, a)
    return a, u


def _scan_tile(a_buf, u_buf, carry_ref, reverse):
    n_slab = a_buf.shape[0]
    order = range(SEG_LEN - 1, -1, -1) if reverse else range(SEG_LEN)
    hs = [jnp.zeros((SUBLANES, LANES), F32) for _ in range(n_slab)]
    cums = [jnp.ones((SUBLANES, LANES), F32) for _ in range(n_slab)]
    for k in order:
        for c in range(n_slab):
            idx = (c, pl.ds(k, SUBLANES, stride=SEG_PITCH), slice(None))
            a = a_buf[idx]
            hs[c] = a * hs[c] + u_buf[idx]
            cums[c] = cums[c] * a
            u_buf[idx] = hs[c]
            a_buf[idx] = cums[c]
    rows = lax.broadcasted_iota(jnp.int32, (SUBLANES, LANES), 0)
    slabs = []
    for c in range(n_slab):
        seg_a, seg_u = _sublane_scan(cums[c], hs[c], reverse)
        tile_in = carry_ref[:, _lanes(c)]
        seg_out = seg_u + seg_a * tile_in
        if reverse:
            seg_in = jnp.where(rows < SUBLANES - 1, pltpu.roll(seg_out, SUBLANES - 1, 0), tile_in)
            carry_ref[:, _lanes(c)] = seg_out[0:1, :]
        else:
            seg_in = jnp.where(rows >= 1, pltpu.roll(seg_out, 1, 0), tile_in)
            carry_ref[:, _lanes(c)] = seg_out[SUBLANES - 1:SUBLANES, :]
        parts = []
        for s in range(SUBLANES):
            seg = pl.ds(s * SEG_PITCH, SEG_LEN)
            parts.append(u_buf[c, seg, :] + a_buf[c, seg, :] * seg_in[s:s + 1, :])
        slabs.append(jnp.concatenate(parts, axis=0))
    return jnp.concatenate(slabs, axis=1)


def _lru_conv(lx_buf, cw_ref, cb_ref, tile, offsets):
    slabs = []
    for c in range(lx_buf.shape[0]):
        xc = cb_ref[:, _lanes(c)]
        for k in range(LRU_CONV_K):
            xc = xc + cw_ref[k:k + 1, _lanes(c)] * lx_buf[c, pl.ds(offsets[k], tile), :]
        slabs.append(xc)
    return jnp.concatenate(slabs, axis=1)


def _lru_bwd_kernel(x_ref, nm_ref, w_ref, cw_ref, cb_ref, wg_ref, ba_ref, bx_ref, lam_ref,
                    out_ref, lx_buf, a_buf, u_buf, carry):
    tile = x_ref.shape[0]
    n_slab = LRU_WIDTH // LANES
    first = pl.program_id(1) == 0

    @pl.when(first)
    def _():
        lx_buf[:, pl.ds(tile, SUBLANES), :] = jnp.zeros((n_slab, SUBLANES, LANES), F32)
        carry[...] = jnp.zeros_like(carry)

    h = _rms(x_ref[...], nm_ref[...]).astype(BF16)
    lx = _dot(h, w_ref[...])
    for c in range(n_slab):
        lx_buf[c, pl.ds(0, tile), :] = lx[:, _lanes(c)]
    xc = _lru_conv(lx_buf, cw_ref, cb_ref, tile, [LRU_CONV_K - 1 - k for k in range(LRU_CONV_K)])
    for c in range(n_slab):
        lx_buf[c, pl.ds(tile, SUBLANES), :] = lx[0:SUBLANES, _lanes(c)]
    a, u = _lru_gates(xc, wg_ref, ba_ref[...], bx_ref[...], lam_ref[...])
    _store_segments(a_buf, a)
    _store_segments(u_buf, u)
    out_ref[...] = _scan_tile(a_buf, u_buf, carry, reverse=True)


def _mixer_kernel(x_ref, xn_ref, hb_ref, nm_ref, w_in_ref,
                  cw_ref, cb_ref, gng_ref, gnb_ref, gm_ref,
                  lcw_ref, lcb_ref, wg_ref, ba_ref, bx_ref, lam_ref,
                  sg_ref, sb_ref, wcat_ref, bs_ref, on_ref, w_out_ref,
                  out_ref, g_buf, lx_buf, a_buf, u_buf, carry):
    tile = x_ref.shape[0]
    step = pl.program_id(1)
    first = step == 0
    last = step == pl.num_programs(1) - 1
    conv_slabs = CONV_WIDTH // LANES
    lru_slabs = LRU_WIDTH // LANES

    @pl.when(first)
    def _():
        g_buf[:, pl.ds(0, CONV_HALO), :] = jnp.zeros((conv_slabs, CONV_HALO, LANES), F32)
        lx_buf[:, pl.ds(0, SUBLANES), :] = jnp.zeros((lru_slabs, SUBLANES, LANES), F32)
        carry[...] = jnp.zeros_like(carry)

    x = x_ref[...]
    nm = nm_ref[...]
    h = _rms(x, nm).astype(BF16)
    z = _dot(h, w_in_ref[...])
    cv = z[:, 0:CONV_WIDTH]
    cg = z[:, CONV_WIDTH:2 * CONV_WIDTH]
    o = 2 * CONV_WIDTH
    lx = z[:, o:o + LRU_WIDTH]
    lg = z[:, o + LRU_WIDTH:o + 2 * LRU_WIDTH]
    o = o + 2 * LRU_WIDTH
    su = z[:, o:o + SGU_WIDTH]
    sv = z[:, o + SGU_WIDTH:o + 2 * SGU_WIDTH]

    hn = _rms(xn_ref[...], nm).astype(BF16)
    zn = _dot(hn, w_in_ref[:, 0:2 * CONV_WIDTH])
    glu = cv * _sigmoid(cg)
    glu_next = jnp.where(last, 0.0, zn[:, 0:CONV_WIDTH] * _sigmoid(zn[:, CONV_WIDTH:]))
    for c in range(conv_slabs):
        g_buf[c, pl.ds(CONV_HALO, tile), :] = glu[:, _lanes(c)]
        g_buf[c, pl.ds(CONV_HALO + tile, CONV_HALO), :] = glu_next[:, _lanes(c)]
    base = CONV_HALO - CONV_K // 2
    slabs = []
    for c in range(conv_slabs):
        parts = []
        for r in range(0, tile, CONV_ROWS):
            acc = cb_ref[:, _lanes(c)]
            for d in range(CONV_K):
                acc = acc + cw_ref[d:d + 1, _lanes(c)] * g_buf[c, pl.ds(base + r + d, CONV_ROWS), :]
            parts.append(acc)
        slabs.append(jnp.concatenate(parts, axis=0))
    conv = jnp.concatenate(slabs, axis=1)
    for c in range(conv_slabs):
        g_buf[c, pl.ds(0, CONV_HALO), :] = glu[tile - CONV_HALO:tile, _lanes(c)]
    gm = gm_ref[...]
    dev = conv - _split_dot(conv, gm)
    var = _split_dot(dev * dev, gm)
    ya = dev * lax.rsqrt(var + EPS) * gng_ref[...] + gnb_ref[...]
    ya = ya * _sigmoid(ya)

    for c in range(lru_slabs):
        lx_buf[c, pl.ds(SUBLANES, tile), :] = lx[:, _lanes(c)]
    xc = _lru_conv(lx_buf, lcw_ref, lcb_ref, tile,
                   [SUBLANES - (LRU_CONV_K - 1 - k) for k in range(LRU_CONV_K)])
    for c in range(lru_slabs):
        lx_buf[c, pl.ds(0, SUBLANES), :] = lx[tile - SUBLANES:tile, _lanes(c)]
    a, u = _lru_gates(xc, wg_ref, ba_ref[...], bx_ref[...], lam_ref[...])
    _store_segments(a_buf, a)
    _store_segments(u_buf, u)
    h_fwd = _scan_tile(a_buf, u_buf, carry, reverse=False)
    yb = (h_fwd + hb_ref[...]) * (0.5 * _gelu2(lg))

    gu = 0.5 * _gelu2(su)
    gv = 0.5 * _gelu2(sv)
    mu = jnp.mean(gv, axis=-1, keepdims=True)
    dv = gv - mu
    v = dv * lax.rsqrt(jnp.mean(dv * dv, axis=-1, keepdims=True) + EPS) * sg_ref[...] + sb_ref[...]
    head = lax.broadcasted_iota(jnp.int32, (CHUNK, SGU_WIDTH), 1) // HEAD_DIM
    wcat = wcat_ref[...]
    bs = bs_ref[...]
    yc_parts = []
    for c in range(tile // CHUNK):
        vc = v[c * CHUNK:(c + 1) * CHUNK, :]
        stack = jnp.concatenate([jnp.where(head == hd, vc, 0.0) for hd in range(SGU_HEADS)], axis=0)
        mixed = _dot(wcat, stack.astype(BF16)) + bs
        yc_parts.append(gu[c * CHUNK:(c + 1) * CHUNK, :] * mixed)
    yc = jnp.concatenate(yc_parts, axis=0)

    on = on_ref[...]
    y = jnp.concatenate([
        _rms(ya, on[:, 0:CONV_WIDTH]),
        _rms(yb, on[:, CONV_WIDTH:CONV_WIDTH + LRU_WIDTH]),
        _rms(yc, on[:, CONV_WIDTH + LRU_WIDTH:]),
    ], axis=1)
    out_ref[...] = x + _dot(y.astype(BF16), w_out_ref[...])


def _ffn_ple_kernel(x_ref, xp_ref, xn_ref, p_ref, nf_ref, w_up_ref, fw_ref, fb_ref, w_dn_ref,
                    np_ref, wpg_ref, bpg_ref, wp_ref, pn_ref, fn_ref,
                    out_ref, g_buf, u_buf, act_buf, *, final):
    tile = x_ref.shape[0]
    step = pl.program_id(1)
    first = step == 0
    last = step == pl.num_programs(1) - 1
    chunk_slabs = FF_CHUNK // LANES

    x = x_ref[...]
    nf = nf_ref[...]
    hb = jnp.concatenate([
        jnp.where(first, 0.0, _rms(xp_ref[...], nf)),
        _rms(x, nf),
        jnp.where(last, 0.0, _rms(xn_ref[...], nf)),
    ], axis=0).astype(BF16)

    for j in range(D_FF // FF_CHUNK):
        g = _dot(hb, w_up_ref[:, j * FF_CHUNK:(j + 1) * FF_CHUNK])
        u = _dot(hb, w_up_ref[:, D_FF + j * FF_CHUNK:D_FF + (j + 1) * FF_CHUNK])
        acts = []
        for c in range(chunk_slabs):
            slot = (j % 2) * chunk_slabs + c
            g_buf[slot] = g[:, _lanes(c)]
            u_buf[slot] = u[:, _lanes(c)]
            lg = slice(j * FF_CHUNK + c * LANES, j * FF_CHUNK + (c + 1) * LANES)
            lu = slice(D_FF + lg.start, D_FF + lg.stop)
            fg = fb_ref[:, lg]
            fu = 0.5 * fb_ref[:, lu]
            for k in range(FFN_CONV_K):
                rows = pl.ds(SUBLANES - FFN_CONV_K // 2 + k, tile)
                fg = fg + fw_ref[k:k + 1, lg] * g_buf[slot, rows, :]
                fu = fu + (0.5 * fw_ref[k:k + 1, lu]) * u_buf[slot, rows, :]
            acts.append((_gelu2(fg) * fu).astype(BF16))
        act_buf[:, j * FF_CHUNK:(j + 1) * FF_CHUNK] = jnp.concatenate(acts, axis=1)

    acc = x + _dot(act_buf[...], w_dn_ref[...])
    gate = _sigmoid(_dot(_rms(acc, np_ref[...]).astype(BF16), wpg_ref[...]) + bpg_ref[...])
    e = _rms(_dot(p_ref[...].astype(BF16), wp_ref[...]), pn_ref[...])
    y = acc + gate * e
    if final:
        y = _rms(y, fn_ref[...])
    out_ref[...] = y


def _const_spec(shape, layer_index):
    nd = len(shape)
    idx = tuple(layer_index) + (0,) * (nd - len(layer_index))
    block = (None,) * len(layer_index) + tuple(shape[len(layer_index):])
    return pl.BlockSpec(block, lambda b, i: idx, pipeline_mode=pl.Buffered(1))


def _block_diag(w):
    n, d, _ = w.shape
    eye = jnp.eye(n, dtype=w.dtype)
    return (eye[:, None, :, None] * w[:, :, None, :]).reshape(n * d, n * d)


def _gate_weights(wa, wx):
    hh = LRU_HEADS // 2
    return jnp.stack([_block_diag(wa[:hh]), _block_diag(wa[hh:]),
                      _block_diag(wx[:hh]), _block_diag(wx[hh:])]).astype(BF16)


def kernel(x, p, norm_mix, w_in, conv_dw_w, conv_dw_b, conv_gn_g, conv_gn_b, lru_conv_w, lru_conv_b, lru_wa, lru_ba, lru_wx, lru_bx, lru_lambda, sgu_ln_g, sgu_ln_b, sgu_ws, sgu_bs, out_norm, w_out, norm_ffn, w_up, ffn_conv_w, ffn_conv_b, w_down, norm_ple, w_ple_gate, b_ple_gate, w_ple, ple_post_norm, final_norm):
    bsz, seq, _ = x.shape
    depth = w_in.shape[0]
    tile = SEQ_TILE
    nt = seq // tile
    assert seq % tile == 0 and tile % CHUNK == 0 and tile % CONV_ROWS == 0

    row = lambda a: a[..., None, :]
    w_in_b = w_in.astype(BF16)
    w_out_b = w_out.astype(BF16)
    w_up_b = w_up.astype(BF16)
    w_dn_b = w_down.astype(BF16)
    wpg_b = w_ple_gate.astype(BF16)
    wp_b = w_ple.astype(BF16)
    wg = jnp.stack([jnp.stack([_gate_weights(lru_wa[l, d], lru_wx[l, d]) for d in range(2)])
                    for l in range(depth)])
    wcat = jnp.transpose(sgu_ws, (0, 2, 1, 3)).reshape(depth, CHUNK, SGU_HEADS * CHUNK).astype(BF16)
    bs_full = jnp.repeat(jnp.transpose(sgu_bs, (0, 2, 1)), HEAD_DIM, axis=2)
    grp = jnp.arange(CONV_WIDTH) // HEAD_DIM
    gmean = ((grp[:, None] == grp[None, :]).astype(F32) / HEAD_DIM).astype(BF16)
    norm_mix_r, out_norm_r, norm_ffn_r, norm_ple_r = row(norm_mix), row(out_norm), row(norm_ffn), row(norm_ple)
    conv_b_r, gn_g_r, gn_b_r = row(conv_dw_b), row(conv_gn_g), row(conv_gn_b)
    lcb_r, ba_r, bx_r, lam_r = row(lru_conv_b), row(lru_ba), row(lru_bx), row(lru_lambda)
    sg_r, sb_r = row(sgu_ln_g), row(sgu_ln_b)
    fb_r, bpg_r, pn_r = row(ffn_conv_b), row(b_ple_gate), row(ple_post_norm)
    fn_r = final_norm[None, :]

    params = pltpu.CompilerParams(dimension_semantics=("arbitrary", "arbitrary"),
                                  vmem_limit_bytes=VMEM_LIMIT)
    x_tile = lambda width, imap: pl.BlockSpec((None, tile, width), imap)
    fwd = lambda b, i: (b, i, 0)
    rev = lambda b, i: (b, nt - 1 - i, 0)
    lru_slabs = LRU_WIDTH // LANES
    seg_scratch = pltpu.VMEM((lru_slabs, SUBLANES * SEG_PITCH, LANES), F32)

    for l in range(depth):
        cs = lambda a, *extra: _const_spec(a.shape, (l,) + extra)

        h_bwd = pl.pallas_call(
            _lru_bwd_kernel,
            grid=(bsz, nt),
            in_specs=[
                x_tile(D_MODEL, rev),
                cs(norm_mix_r),
                pl.BlockSpec((None, D_MODEL, LRU_WIDTH), lambda b, i: (l, 0, 1),
                             pipeline_mode=pl.Buffered(1)),
                cs(lru_conv_w, 1), cs(lcb_r, 1), cs(wg, 1), cs(ba_r, 1), cs(bx_r, 1), cs(lam_r, 1),
            ],
            out_specs=x_tile(LRU_WIDTH, rev),
            out_shape=jax.ShapeDtypeStruct((bsz, seq, LRU_WIDTH), F32),
            scratch_shapes=[
                pltpu.VMEM((lru_slabs, tile + SUBLANES, LANES), F32),
                seg_scratch, seg_scratch,
                pltpu.VMEM((1, LRU_WIDTH), F32),
            ],
            compiler_params=params,
            name="lru_bwd",
        )(x, norm_mix_r, w_in_b, lru_conv_w, lcb_r, wg, ba_r, bx_r, lam_r)

        halo_blocks = tile // CONV_HALO
        n_halo = seq // CONV_HALO
        x = pl.pallas_call(
            _mixer_kernel,
            grid=(bsz, nt),
            in_specs=[
                x_tile(D_MODEL, fwd),
                pl.BlockSpec((None, CONV_HALO, D_MODEL),
                             lambda b, i: (b, jnp.minimum((i + 1) * halo_blocks, n_halo - 1), 0)),
                x_tile(LRU_WIDTH, fwd),
                cs(norm_mix_r), cs(w_in_b),
                cs(conv_dw_w), cs(conv_b_r), cs(gn_g_r), cs(gn_b_r),
                pl.BlockSpec(gmean.shape, lambda b, i: (0, 0), pipeline_mode=pl.Buffered(1)),
                cs(lru_conv_w, 0), cs(lcb_r, 0), cs(wg, 0), cs(ba_r, 0), cs(bx_r, 0), cs(lam_r, 0),
                cs(sg_r), cs(sb_r), cs(wcat), cs(bs_full), cs(out_norm_r), cs(w_out_b),
            ],
            out_specs=x_tile(D_MODEL, fwd),
            out_shape=jax.ShapeDtypeStruct((bsz, seq, D_MODEL), F32),
            scratch_shapes=[
                pltpu.VMEM((CONV_WIDTH // LANES, tile + 2 * CONV_HALO, LANES), F32),
                pltpu.VMEM((lru_slabs, tile + SUBLANES, LANES), F32),
                seg_scratch, seg_scratch,
                pltpu.VMEM((1, LRU_WIDTH), F32),
            ],
            compiler_params=params,
            name="mixer",
        )(x, x, h_bwd, norm_mix_r, w_in_b, conv_dw_w, conv_b_r, gn_g_r, gn_b_r, gmean,
          lru_conv_w, lcb_r, wg, ba_r, bx_r, lam_r, sg_r, sb_r, wcat, bs_full, out_norm_r, w_out_b)

        rows8 = tile // SUBLANES
        n8 = seq // SUBLANES
        ff_scratch = pltpu.VMEM((2 * FF_CHUNK // LANES, tile + 2 * SUBLANES, LANES), F32)
        x = pl.pallas_call(
            functools.partial(_ffn_ple_kernel, final=(l == depth - 1)),
            grid=(bsz, nt),
            in_specs=[
                x_tile(D_MODEL, fwd),
                pl.BlockSpec((None, SUBLANES, D_MODEL),
                             lambda b, i: (b, jnp.maximum(i * rows8 - 1, 0), 0)),
                pl.BlockSpec((None, SUBLANES, D_MODEL),
                             lambda b, i: (b, jnp.minimum((i + 1) * rows8, n8 - 1), 0)),
                pl.BlockSpec((None, None, tile, PLE_DIM), lambda b, i: (l, b, i, 0)),
                cs(norm_ffn_r), cs(w_up_b), cs(ffn_conv_w), cs(fb_r), cs(w_dn_b),
                cs(norm_ple_r), cs(wpg_b), cs(bpg_r), cs(wp_b), cs(pn_r),
                pl.BlockSpec(fn_r.shape, lambda b, i: (0, 0), pipeline_mode=pl.Buffered(1)),
            ],
            out_specs=x_tile(D_MODEL, fwd),
            out_shape=jax.ShapeDtypeStruct((bsz, seq, D_MODEL), F32),
            scratch_shapes=[
                ff_scratch, ff_scratch,
                pltpu.VMEM((tile, D_FF), BF16),
            ],
            compiler_params=params,
            name="ffn_ple",
        )(x, x, x, p, norm_ffn_r, w_up_b, ffn_conv_w, fb_r, w_dn_b,
          norm_ple_r, wpg_b, bpg_r, wp_b, pn_r, fn_r)
    return x
```

```python
import functools

import jax
import jax.numpy as jnp
from jax import lax
from jax.experimental import pallas as pl
from jax.experimental.pallas import tpu as pltpu

D_MODEL = 1024
PLE_DIM = 256
HEAD_DIM = 64
CONV_WIDTH = 256
LRU_WIDTH = 512
LRU_HEADS = LRU_WIDTH // HEAD_DIM
SGU_WIDTH = 256
SGU_HEADS = SGU_WIDTH // HEAD_DIM
MIX_WIDTH = CONV_WIDTH + LRU_WIDTH + SGU_WIDTH
IN_WIDTH = 2 * MIX_WIDTH
CONV_K = 31
LRU_CONV_K = 4
LRU_C = 8.0
CHUNK = 128
D_FF = 2816
FFN_CONV_K = 3
EPS = 1e-6

LANES = 128
SUBLANES = 8
SEQ_TILE = 512
SEG_LEN = SEQ_TILE // SUBLANES
SEG_PITCH = SEG_LEN + 4
CONV_HALO = 16
CONV_ROWS = 128
FF_CHUNK = 256
VMEM_LIMIT = 56 * 1024 * 1024

F32 = jnp.float32
BF16 = jnp.bfloat16


def _lanes(c):
    return slice(c * LANES, (c + 1) * LANES)


def _gelu2(x):
    t = jnp.tanh(x * (0.7978845608028654 + (0.7978845608028654 * 0.044715) * (x * x)))
    return x + x * t


def _rms(x, g):
    return x * lax.rsqrt(jnp.mean(x * x, axis=-1, keepdims=True) + EPS) * g


def _dot(a, b):
    return jnp.dot(a, b, preferred_element_type=F32)


def _split_dot(x, m):
    hi = x.astype(BF16)
    lo = (x - hi.astype(F32)).astype(BF16)
    return _dot(hi, m) + _dot(lo, m)


def _log_sigmoid(x):
    y = jnp.exp(-jnp.abs(x))
    log1p_y = jnp.where(y < 1e-4, y * (1.0 - 0.5 * y), jnp.log(1.0 + y))
    return jnp.minimum(x, 0.0) - log1p_y


def _lru_gates(xc, wg_ref, ba, bx, lam):
    xb = xc.astype(BF16)
    half = LRU_WIDTH // 2
    x0, x1 = xb[:, :half], xb[:, half:]
    th_r = jnp.tanh(jnp.concatenate([_dot(x0, wg_ref[0]), _dot(x1, wg_ref[1])], axis=1) + 0.5 * ba)
    th_i = jnp.tanh(jnp.concatenate([_dot(x0, wg_ref[2]), _dot(x1, wg_ref[3])], axis=1) + 0.5 * bx)
    c_half = (0.5 * LRU_C) * _log_sigmoid(lam)
    log_a = c_half * th_r + c_half
    a = jnp.exp(log_a)
    tl = jnp.tanh(log_a)
    y = tl / (tl - 1.0)
    root = y * lax.rsqrt(jnp.maximum(y, 1e-30))
    u = (xc * th_i + xc) * (root * 0.7071067811865476)
    return a, u


def _store_segments(buf, val):
    for c in range(buf.shape[0]):
        for s in range(SUBLANES):
            buf[c, pl.ds(s * SEG_PITCH, SEG_LEN), :] = val[s * SEG_LEN:(s + 1) * SEG_LEN, _lanes(c)]


def _sublane_scan(a, u, reverse):
    rows = lax.broadcasted_iota(jnp.int32, a.shape, 0)
    for s in (1, 2, 4):
        shift = (SUBLANES - s) if reverse else s
        a_sh = pltpu.roll(a, shift, 0)
        u_sh = pltpu.roll(u, shift, 0)
        m = (rows < SUBLANES - s) if reverse else (rows >= s)
        u = jnp.where(m, a * u_sh + u, u)
        a = jnp.where(m, a * a_sh, a)
    return a, u


def _scan_tile(a_buf, u_buf, carry_ref, reverse):
    n_slab = a_buf.shape[0]
    order = range(SEG_LEN - 1, -1, -1) if reverse else range(SEG_LEN)
    hs = [jnp.zeros((SUBLANES, LANES), F32) for _ in range(n_slab)]
    cums = [jnp.ones((SUBLANES, LANES), F32) for _ in range(n_slab)]
    for k in order:
        for c in range(n_slab):
            idx = (c, pl.ds(k, SUBLANES, stride=SEG_PITCH), slice(None))
            a = a_buf[idx]
            hs[c] = a * hs[c] + u_buf[idx]
            cums[c] = cums[c] * a
            u_buf[idx] = hs[c]
            a_buf[idx] = cums[c]
    rows = lax.broadcasted_iota(jnp.int32, (SUBLANES, LANES), 0)
    slabs = []
    for c in range(n_slab):
        seg_a, seg_u = _sublane_scan(cums[c], hs[c], reverse)
        tile_in = carry_ref[:, _lanes(c)]
        seg_out = seg_u + seg_a * tile_in
        if reverse:
            seg_in = jnp.where(rows < SUBLANES - 1, pltpu.roll(seg_out, SUBLANES - 1, 0), tile_in)
            carry_ref[:, _lanes(c)] = seg_out[0:1, :]
        else:
            seg_in = jnp.where(rows >= 1, pltpu.roll(seg_out, 1, 0), tile_in)
            carry_ref[:, _lanes(c)] = seg_out[SUBLANES - 1:SUBLANES, :]
        parts = []
        for s in range(SUBLANES):
            seg = pl.ds(s * SEG_PITCH, SEG_LEN)
            parts.append(u_buf[c, seg, :] + a_buf[c, seg, :] * seg_in[s:s + 1, :])
        slabs.append(jnp.concatenate(parts, axis=0))
    return jnp.concatenate(slabs, axis=1)


def _lru_conv(lx_buf, cw_ref, cb, tile, offsets):
    slabs = []
    for c in range(lx_buf.shape[0]):
        xc = cb[:, _lanes(c)]
        for k in range(LRU_CONV_K):
            xc = xc + cw_ref[k:k + 1, _lanes(c)] * lx_buf[c, pl.ds(offsets[k], tile), :]
        slabs.append(xc)
    return jnp.concatenate(slabs, axis=1)


def _lru_bwd_kernel(x_ref, nm_ref, w_ref, cw_ref, cb_ref, wg_ref, ba_ref, bx_ref, lam_ref,
                    out_ref, lx_buf, a_buf, u_buf, carry, *, layer):
    tile = x_ref.shape[0]
    n_slab = LRU_WIDTH // LANES
    vec = lambda ref: ref[layer, 1:2, :]
    first = pl.program_id(1) == 0

    @pl.when(first)
    def _():
        lx_buf[:, pl.ds(tile, SUBLANES), :] = jnp.zeros((n_slab, SUBLANES, LANES), F32)
        carry[...] = jnp.zeros_like(carry)

    h = _rms(x_ref[...], nm_ref[layer:layer + 1, :]).astype(BF16)
    lx = _dot(h, w_ref[...])
    for c in range(n_slab):
        lx_buf[c, pl.ds(0, tile), :] = lx[:, _lanes(c)]
    xc = _lru_conv(lx_buf, cw_ref, vec(cb_ref), tile, [LRU_CONV_K - 1 - k for k in range(LRU_CONV_K)])
    for c in range(n_slab):
        lx_buf[c, pl.ds(tile, SUBLANES), :] = lx[0:SUBLANES, _lanes(c)]
    a, u = _lru_gates(xc, wg_ref, vec(ba_ref), vec(bx_ref), vec(lam_ref))
    _store_segments(a_buf, a)
    _store_segments(u_buf, u)
    out_ref[...] = _scan_tile(a_buf, u_buf, carry, reverse=True)


def _mixer_kernel(x_ref, xn_ref, hb_ref, nm_ref, w_in_ref,
                  cw_ref, cb_ref, gng_ref, gnb_ref, gm_ref,
                  lcw_ref, lcb_ref, wg_ref, ba_ref, bx_ref, lam_ref,
                  sg_ref, sb_ref, wcat_ref, bs_ref, on_ref, w_out_ref,
                  out_ref, g_buf, lx_buf, a_buf, u_buf, carry, *, layer):
    tile = x_ref.shape[0]
    step = pl.program_id(1)
    first = step == 0
    last = step == pl.num_programs(1) - 1
    conv_slabs = CONV_WIDTH // LANES
    lru_slabs = LRU_WIDTH // LANES
    row = lambda ref: ref[layer:layer + 1, :]
    vec = lambda ref: ref[layer, 0:1, :]

    @pl.when(first)
    def _():
        g_buf[:, pl.ds(0, CONV_HALO), :] = jnp.zeros((conv_slabs, CONV_HALO, LANES), F32)
        lx_buf[:, pl.ds(0, SUBLANES), :] = jnp.zeros((lru_slabs, SUBLANES, LANES), F32)
        carry[...] = jnp.zeros_like(carry)

    x = x_ref[...]
    nm = row(nm_ref)
    h = _rms(x, nm).astype(BF16)
    z = _dot(h, w_in_ref[...])
    cv_half = z[:, 0:CONV_WIDTH]
    cg_half = z[:, CONV_WIDTH:2 * CONV_WIDTH]
    o = 2 * CONV_WIDTH
    lx = z[:, o:o + LRU_WIDTH]
    lg = z[:, o + LRU_WIDTH:o + 2 * LRU_WIDTH]
    o = o + 2 * LRU_WIDTH
    su = z[:, o:o + SGU_WIDTH]
    sv = z[:, o + SGU_WIDTH:o + 2 * SGU_WIDTH]

    hn = _rms(xn_ref[...], nm).astype(BF16)
    zn = _dot(hn, w_in_ref[:, 0:2 * CONV_WIDTH])
    glu = cv_half * jnp.tanh(cg_half) + cv_half
    vn_half = zn[:, 0:CONV_WIDTH]
    glu_next = jnp.where(last, 0.0, vn_half * jnp.tanh(zn[:, CONV_WIDTH:]) + vn_half)
    for c in range(conv_slabs):
        g_buf[c, pl.ds(CONV_HALO, tile), :] = glu[:, _lanes(c)]
        g_buf[c, pl.ds(CONV_HALO + tile, CONV_HALO), :] = glu_next[:, _lanes(c)]
    base = CONV_HALO - CONV_K // 2
    cb = row(cb_ref)
    slabs = []
    for c in range(conv_slabs):
        parts = []
        for r in range(0, tile, CONV_ROWS):
            acc = cb[:, _lanes(c)]
            for d in range(CONV_K):
                acc = acc + cw_ref[d:d + 1, _lanes(c)] * g_buf[c, pl.ds(base + r + d, CONV_ROWS), :]
            parts.append(acc)
        slabs.append(jnp.concatenate(parts, axis=0))
    conv = jnp.concatenate(slabs, axis=1)
    for c in range(conv_slabs):
        g_buf[c, pl.ds(0, CONV_HALO), :] = glu[tile - CONV_HALO:tile, _lanes(c)]
    gm = gm_ref[...]
    dev = conv - _split_dot(conv, gm)
    var = _split_dot(dev * dev, gm)
    ya_half = dev * (0.5 * lax.rsqrt(var + EPS)) * row(gng_ref) + 0.5 * row(gnb_ref)
    ya = ya_half * jnp.tanh(ya_half) + ya_half

    for c in range(lru_slabs):
        lx_buf[c, pl.ds(SUBLANES, tile), :] = lx[:, _lanes(c)]
    xc = _lru_conv(lx_buf, lcw_ref, vec(lcb_ref), tile,
                   [SUBLANES - (LRU_CONV_K - 1 - k) for k in range(LRU_CONV_K)])
    for c in range(lru_slabs):
        lx_buf[c, pl.ds(0, SUBLANES), :] = lx[tile - SUBLANES:tile, _lanes(c)]
    a, u = _lru_gates(xc, wg_ref, vec(ba_ref), vec(bx_ref), vec(lam_ref))
    _store_segments(a_buf, a)
    _store_segments(u_buf, u)
    h_fwd = _scan_tile(a_buf, u_buf, carry, reverse=False)
    yb = (h_fwd + hb_ref[...]) * (0.5 * _gelu2(lg))

    gu = 0.5 * _gelu2(su)
    gv = 0.5 * _gelu2(sv)
    mu = jnp.mean(gv, axis=-1, keepdims=True)
    dv = gv - mu
    v = dv * lax.rsqrt(jnp.mean(dv * dv, axis=-1, keepdims=True) + EPS) * row(sg_ref) + row(sb_ref)
    head = lax.broadcasted_iota(jnp.int32, (CHUNK, SGU_WIDTH), 1) // HEAD_DIM
    wcat = wcat_ref[...]
    bs = bs_ref[...]
    yc_parts = []
    for c in range(tile // CHUNK):
        vc = v[c * CHUNK:(c + 1) * CHUNK, :]
        stack = jnp.concatenate([jnp.where(head == hd, vc, 0.0) for hd in range(SGU_HEADS)], axis=0)
        mixed = _dot(wcat, stack.astype(BF16)) + bs
        yc_parts.append(gu[c * CHUNK:(c + 1) * CHUNK, :] * mixed)
    yc = jnp.concatenate(yc_parts, axis=0)

    on = row(on_ref)
    y = jnp.concatenate([
        _rms(ya, on[:, 0:CONV_WIDTH]),
        _rms(yb, on[:, CONV_WIDTH:CONV_WIDTH + LRU_WIDTH]),
        _rms(yc, on[:, CONV_WIDTH + LRU_WIDTH:]),
    ], axis=1)
    out_ref[...] = x + _dot(y.astype(BF16), w_out_ref[...])


def _ffn_ple_kernel(x_ref, xp_ref, xn_ref, p_ref, nf_ref, w_up_ref, fw_ref, fb_ref, w_dn_ref,
                    np_ref, wpg_ref, bpg_ref, wp_ref, pn_ref, fn_ref,
                    out_ref, g_buf, u_buf, act_buf, *, layer, final):
    tile = x_ref.shape[0]
    step = pl.program_id(1)
    first = step == 0
    last = step == pl.num_programs(1) - 1
    chunk_slabs = FF_CHUNK // LANES
    row = lambda ref: ref[layer:layer + 1, :]

    x = x_ref[...]
    nf = row(nf_ref)
    hb = jnp.concatenate([
        jnp.where(first, 0.0, _rms(xp_ref[...], nf)),
        _rms(x, nf),
        jnp.where(last, 0.0, _rms(xn_ref[...], nf)),
    ], axis=0).astype(BF16)

    for j in range(D_FF // FF_CHUNK):
        g = _dot(hb, w_up_ref[:, j * FF_CHUNK:(j + 1) * FF_CHUNK])
        u = _dot(hb, w_up_ref[:, D_FF + j * FF_CHUNK:D_FF + (j + 1) * FF_CHUNK])
        acts = []
        for c in range(chunk_slabs):
            slot = (j % 2) * chunk_slabs + c
            g_buf[slot] = g[:, _lanes(c)]
            u_buf[slot] = u[:, _lanes(c)]
            lg = slice(j * FF_CHUNK + c * LANES, j * FF_CHUNK + (c + 1) * LANES)
            lu = slice(D_FF + lg.start, D_FF + lg.stop)
            fg = fb_ref[layer:layer + 1, lg]
            fu = 0.5 * fb_ref[layer:layer + 1, lu]
            for k in range(FFN_CONV_K):
                rows = pl.ds(SUBLANES - FFN_CONV_K // 2 + k, tile)
                fg = fg + fw_ref[k:k + 1, lg] * g_buf[slot, rows, :]
                fu = fu + (0.5 * fw_ref[k:k + 1, lu]) * u_buf[slot, rows, :]
            acts.append((_gelu2(fg) * fu).astype(BF16))
        act_buf[:, j * FF_CHUNK:(j + 1) * FF_CHUNK] = jnp.concatenate(acts, axis=1)

    acc = x + _dot(act_buf[...], w_dn_ref[...])
    th = jnp.tanh(_dot(_rms(acc, row(np_ref)).astype(BF16), wpg_ref[...]) + 0.5 * row(bpg_ref))
    e_half = _rms(_dot(p_ref[...].astype(BF16), wp_ref[...]), 0.5 * row(pn_ref))
    y = acc + (e_half * th + e_half)
    if final:
        y = _rms(y, fn_ref[...])
    out_ref[...] = y


def _layer_spec(a, index):
    idx = tuple(index) + (0,) * (a.ndim - len(index))
    block = (None,) * len(index) + tuple(a.shape[len(index):])
    return pl.BlockSpec(block, lambda b, i: idx, pipeline_mode=pl.Buffered(1))


def _whole_spec(a):
    idx = (0,) * a.ndim
    return pl.BlockSpec(a.shape, lambda b, i: idx, pipeline_mode=pl.Buffered(1))


def _prepare_weights(w_in, lru_wa, lru_wx, sgu_ws, sgu_bs, w_ple_gate):
    depth = w_in.shape[0]
    col = jnp.arange(IN_WIDTH)
    w_in_b = (w_in * jnp.where(col < 2 * CONV_WIDTH, 0.5, 1.0)).astype(BF16)
    wpg_b = (0.5 * w_ple_gate).astype(BF16)
    hh = LRU_HEADS // 2
    w = jnp.stack([lru_wa, lru_wx], axis=2).reshape(depth, 2, 2, 2, hh, HEAD_DIM, HEAD_DIM)
    eye = jnp.eye(hh, dtype=F32)
    wg = (0.5 * w[..., :, :, None, :] * eye[:, None, :, None]).reshape(
        depth, 2, 4, hh * HEAD_DIM, hh * HEAD_DIM).astype(BF16)
    wcat = jnp.transpose(sgu_ws, (0, 2, 1, 3)).reshape(depth, CHUNK, SGU_HEADS * CHUNK).astype(BF16)
    bs_full = jnp.repeat(jnp.transpose(sgu_bs, (0, 2, 1)), HEAD_DIM, axis=2)
    grp = jnp.arange(CONV_WIDTH) // HEAD_DIM
    gmean = ((grp[:, None] == grp[None, :]).astype(F32) / HEAD_DIM).astype(BF16)
    return w_in_b, wpg_b, wg, wcat, bs_full, gmean


def kernel(x, p, norm_mix, w_in, conv_dw_w, conv_dw_b, conv_gn_g, conv_gn_b, lru_conv_w, lru_conv_b, lru_wa, lru_ba, lru_wx, lru_bx, lru_lambda, sgu_ln_g, sgu_ln_b, sgu_ws, sgu_bs, out_norm, w_out, norm_ffn, w_up, ffn_conv_w, ffn_conv_b, w_down, norm_ple, w_ple_gate, b_ple_gate, w_ple, ple_post_norm, final_norm):
    bsz, seq, _ = x.shape
    depth = w_in.shape[0]
    tile = SEQ_TILE
    nt = seq // tile
    assert seq % tile == 0 and tile % CHUNK == 0 and tile % CONV_ROWS == 0

    w_in_b, wpg_b, wg, wcat, bs_full, gmean = _prepare_weights(w_in, lru_wa, lru_wx, sgu_ws, sgu_bs,
                                                               w_ple_gate)
    w_out_b = w_out.astype(BF16)
    w_up_b = w_up.astype(BF16)
    w_dn_b = w_down.astype(BF16)
    wp_b = w_ple.astype(BF16)
    fn_r = final_norm[None, :]

    params = pltpu.CompilerParams(dimension_semantics=("arbitrary", "arbitrary"),
                                  vmem_limit_bytes=VMEM_LIMIT)
    x_tile = lambda width, imap: pl.BlockSpec((None, tile, width), imap)
    fwd = lambda b, i: (b, i, 0)
    rev = lambda b, i: (b, nt - 1 - i, 0)
    lru_slabs = LRU_WIDTH // LANES
    seg_scratch = pltpu.VMEM((lru_slabs, SUBLANES * SEG_PITCH, LANES), F32)
    whole = _whole_spec

    for l in range(depth):
        layer = lambda a, *extra: _layer_spec(a, (l,) + extra)

        h_bwd = pl.pallas_call(
            functools.partial(_lru_bwd_kernel, layer=l),
            grid=(bsz, nt),
            in_specs=[
                x_tile(D_MODEL, rev),
                whole(norm_mix),
                pl.BlockSpec((None, D_MODEL, LRU_WIDTH), lambda b, i: (l, 0, 1),
                             pipeline_mode=pl.Buffered(1)),
                layer(lru_conv_w, 1), whole(lru_conv_b), layer(wg, 1),
                whole(lru_ba), whole(lru_bx), whole(lru_lambda),
            ],
            out_specs=x_tile(LRU_WIDTH, rev),
            out_shape=jax.ShapeDtypeStruct((bsz, seq, LRU_WIDTH), F32),
            scratch_shapes=[
                pltpu.VMEM((lru_slabs, tile + SUBLANES, LANES), F32),
                seg_scratch, seg_scratch,
                pltpu.VMEM((1, LRU_WIDTH), F32),
            ],
            compiler_params=params,
            name="lru_bwd",
        )(x, norm_mix, w_in_b, lru_conv_w, lru_conv_b, wg, lru_ba, lru_bx, lru_lambda)

        halo_blocks = tile // CONV_HALO
        n_halo = seq // CONV_HALO
        x = pl.pallas_call(
            functools.partial(_mixer_kernel, layer=l),
            grid=(bsz, nt),
            in_specs=[
                x_tile(D_MODEL, fwd),
                pl.BlockSpec((None, CONV_HALO, D_MODEL),
                             lambda b, i: (b, jnp.minimum((i + 1) * halo_blocks, n_halo - 1), 0)),
                x_tile(LRU_WIDTH, fwd),
                whole(norm_mix), layer(w_in_b),
                layer(conv_dw_w), whole(conv_dw_b), whole(conv_gn_g), whole(conv_gn_b), whole(gmean),
                layer(lru_conv_w, 0), whole(lru_conv_b), layer(wg, 0),
                whole(lru_ba), whole(lru_bx), whole(lru_lambda),
                whole(sgu_ln_g), whole(sgu_ln_b), layer(wcat), layer(bs_full),
                whole(out_norm), layer(w_out_b),
            ],
            out_specs=x_tile(D_MODEL, fwd),
            out_shape=jax.ShapeDtypeStruct((bsz, seq, D_MODEL), F32),
            scratch_shapes=[
                pltpu.VMEM((CONV_WIDTH // LANES, tile + 2 * CONV_HALO, LANES), F32),
                pltpu.VMEM((lru_slabs, tile + SUBLANES, LANES), F32),
                seg_scratch, seg_scratch,
                pltpu.VMEM((1, LRU_WIDTH), F32),
            ],
            compiler_params=params,
            name="mixer",
        )(x, x, h_bwd, norm_mix, w_in_b, conv_dw_w, conv_dw_b, conv_gn_g, conv_gn_b, gmean,
          lru_conv_w, lru_conv_b, wg, lru_ba, lru_bx, lru_lambda,
          sgu_ln_g, sgu_ln_b, wcat, bs_full, out_norm, w_out_b)

        rows8 = tile // SUBLANES
        n8 = seq // SUBLANES
        ff_scratch = pltpu.VMEM((2 * FF_CHUNK // LANES, tile + 2 * SUBLANES, LANES), F32)
        x = pl.pallas_call(
            functools.partial(_ffn_ple_kernel, layer=l, final=(l == depth - 1)),
            grid=(bsz, nt),
            in_specs=[
                x_tile(D_MODEL, fwd),
                pl.BlockSpec((None, SUBLANES, D_MODEL),
                             lambda b, i: (b, jnp.maximum(i * rows8 - 1, 0), 0)),
                pl.BlockSpec((None, SUBLANES, D_MODEL),
                             lambda b, i: (b, jnp.minimum((i + 1) * rows8, n8 - 1), 0)),
                pl.BlockSpec((None, None, tile, PLE_DIM), lambda b, i: (l, b, i, 0)),
                whole(norm_ffn), layer(w_up_b), layer(ffn_conv_w), whole(ffn_conv_b), layer(w_dn_b),
                whole(norm_ple), layer(wpg_b), whole(b_ple_gate), layer(wp_b), whole(ple_post_norm),
                whole(fn_r),
            ],
            out_specs=x_tile(D_MODEL, fwd),
            out_shape=jax.ShapeDtypeStruct((bsz, seq, D_MODEL), F32),
            scratch_shapes=[
                ff_scratch, ff_scratch,
                pltpu.VMEM((tile, D_FF), BF16),
            ],
            compiler_params=params,
            name="ffn_ple",
        )(x, x, x, p, norm_ffn, w_up_b, ffn_conv_w, ffn_conv_b, w_dn_b,
          norm_ple, wpg_b, b_ple_gate, wp_b, ple_post_norm, fn_r)
    return x
```
